```python
import jax, jax.numpy as jnp
from jax import lax
import numpy as np

D_MODEL = 4096
BATCH = 2
SEQ = 8192
DEPTH = 2

CTX_LEN = 256
GRID_W = 64
N_MIXERS = 2
N_HEADS = 32
N_KV_HEADS = 8
HEAD_DIM = D_MODEL // N_HEADS
GROUPS = N_HEADS // N_KV_HEADS
ROPE_AXIS_DIM = HEAD_DIM // 2
ROPE_THETA = 10000.0
Q_BLOCK = 128
ATTN_SCALE = HEAD_DIM ** -0.5
D_FF = 10240
CONV_WIDTH = 31
CONV_PAD = CONV_WIDTH // 2
N_ADA = 9
N_ATTN_LAYERS = (DEPTH + 1) // 2
N_CONV_LAYERS = DEPTH // 2
FFN_RESIDUAL = 0.5
EPS = 1e-6

kernel_name = 'hybrid_attn_conformer_macaron_dit'


def rms_norm(x, g):
    x32 = x.astype(jnp.float32)
    y = x32 * lax.rsqrt(jnp.mean(x32 * x32, axis=-1, keepdims=True) + EPS)
    return (y * g.astype(jnp.float32)).astype(x.dtype)


def layer_norm(x, g, b):
    x32 = x.astype(jnp.float32)
    mu = jnp.mean(x32, axis=-1, keepdims=True)
    xc = x32 - mu
    y = xc * lax.rsqrt(jnp.mean(xc * xc, axis=-1, keepdims=True) + EPS)
    return (y * g.astype(jnp.float32) + b.astype(jnp.float32)).astype(x.dtype)


def modulate(h, shift, scale):
    return h * (1 + scale) + shift


def axial_rope_tables(rows, dtype):
    row = jnp.repeat(jnp.arange(rows), GRID_W).astype(jnp.float32)
    col = jnp.tile(jnp.arange(GRID_W), rows).astype(jnp.float32)
    inv_freq = 1.0 / (ROPE_THETA ** (jnp.arange(0, ROPE_AXIS_DIM, 2, dtype=jnp.float32) / ROPE_AXIS_DIM))
    ang_r = row[:, None] * inv_freq[None, :]
    ang_c = col[:, None] * inv_freq[None, :]
    ang = jnp.concatenate([ang_r, ang_r, ang_c, ang_c], axis=-1)
    return jnp.cos(ang).astype(dtype), jnp.sin(ang).astype(dtype)


def apply_axial_rope(x, cos, sin):
    xr = x.reshape(x.shape[:-1] + (2, 2, ROPE_AXIS_DIM // 2))
    rot = jnp.stack([-xr[..., 1, :], xr[..., 0, :]], axis=-2).reshape(x.shape)
    return x * cos + rot * sin


def ffn_half(x, shift, scale, gate, g, w_gate, w_up, w_down):
    h = modulate(rms_norm(x, g), shift, scale)
    y = (jax.nn.silu(h @ w_gate) * (h @ w_up)) @ w_down
    return x + FFN_RESIDUAL * gate * y


def attention_mixer(h, hc, wq, wk, wv, wo, q_g, k_g, cos, sin, with_ctx_out):
    B, S, _ = h.shape
    C = hc.shape[1]
    q = rms_norm((h @ wq).reshape(B, S, N_KV_HEADS, GROUPS, HEAD_DIM), q_g)
    k = rms_norm((h @ wk).reshape(B, S, N_KV_HEADS, HEAD_DIM), k_g)
    v = (h @ wv).reshape(B, S, N_KV_HEADS, HEAD_DIM)
    q = apply_axial_rope(q, cos[:, None, None, :], sin[:, None, None, :])
    k = apply_axial_rope(k, cos[:, None, :], sin[:, None, :])
    kc = rms_norm((hc @ wk).reshape(B, C, N_KV_HEADS, HEAD_DIM), k_g)
    vc = (hc @ wv).reshape(B, C, N_KV_HEADS, HEAD_DIM)
    k_all = jnp.concatenate([k, kc], axis=1)
    v_all = jnp.concatenate([v, vc], axis=1)

    n_blk = S // Q_BLOCK
    qb = q.reshape(B, n_blk, Q_BLOCK, N_KV_HEADS, GROUPS, HEAD_DIM).transpose(1, 0, 2, 3, 4, 5)

    def attend_block(q_blk):
        s = jnp.einsum('bqkgd,btkd->bkgqt', q_blk, k_all).astype(jnp.float32) * ATTN_SCALE
        p = jax.nn.softmax(s, axis=-1).astype(v_all.dtype)
        return jnp.einsum('bkgqt,btkd->bqkgd', p, v_all)

    o = lax.map(attend_block, qb)
    o = o.transpose(1, 0, 2, 3, 4, 5).reshape(B, S, D_MODEL)
    y_lat = o @ wo
    y_ctx = None
    if with_ctx_out:
        qc = rms_norm((hc @ wq).reshape(B, C, N_KV_HEADS, GROUPS, HEAD_DIM), q_g)
        s = jnp.einsum('bqkgd,btkd->bkgqt', qc, kc).astype(jnp.float32) * ATTN_SCALE
        p = jax.nn.softmax(s, axis=-1).astype(vc.dtype)
        oc = jnp.einsum('bkgqt,btkd->bqkgd', p, vc).reshape(B, C, D_MODEL)
        y_ctx = oc @ wo
    return y_lat, y_ctx


def conformer_conv(h, w1, b1, w_dw, b_dw, ln_g, ln_b, w2, b2):
    a, g = jnp.split(h @ w1 + b1, 2, axis=-1)
    u = a * jax.nn.sigmoid(g)
    u = lax.conv_general_dilated(
        u, w_dw[:, None, :].astype(u.dtype), window_strides=(1,), padding=[(CONV_PAD, CONV_PAD)],
        dimension_numbers=('NWC', 'WIO', 'NWC'), feature_group_count=D_MODEL) + b_dw
    u = jax.nn.silu(layer_norm(u, ln_g, ln_b))
    return u @ w2 + b2


def setup_inputs(seed: int = 0) -> dict:
    key = jax.random.key(seed)
    ks = jax.random.split(key, 24)

    def nrm(k, shape, scale=1.0):
        return jax.random.normal(k, shape, jnp.float32) * scale

    d = D_MODEL
    return {
        'x': nrm(ks[0], (BATCH, SEQ, d)),
        'c': nrm(ks[1], (BATCH, d)),
        'ctx': nrm(ks[2], (BATCH, CTX_LEN, d)),
        'c_ctx': nrm(ks[3], (d,)),
        'ada_w': nrm(ks[4], (DEPTH, d, N_ADA * d), d ** -0.5),
        'ada_b': nrm(ks[5], (DEPTH, N_ADA * d), 0.02),
        'norm_g': 1.0 + nrm(ks[6], (DEPTH, 3, d), 0.02),
        'ffn_w_gate': nrm(ks[7], (DEPTH, 2, d, D_FF), d ** -0.5),
        'ffn_w_up': nrm(ks[8], (DEPTH, 2, d, D_FF), d ** -0.5),
        'ffn_w_down': nrm(ks[9], (DEPTH, 2, D_FF, d), D_FF ** -0.5),
        'attn_wq': nrm(ks[10], (N_ATTN_LAYERS, d, N_HEADS * HEAD_DIM), d ** -0.5),
        'attn_wk': nrm(ks[11], (N_ATTN_LAYERS, d, N_KV_HEADS * HEAD_DIM), d ** -0.5),
        'attn_wv': nrm(ks[12], (N_ATTN_LAYERS, d, N_KV_HEADS * HEAD_DIM), d ** -0.5),
        'attn_wo': nrm(ks[13], (N_ATTN_LAYERS, N_HEADS * HEAD_DIM, d), d ** -0.5),
        'attn_q_g': 1.0 + nrm(ks[14], (N_ATTN_LAYERS, HEAD_DIM), 0.02),
        'attn_k_g': 1.0 + nrm(ks[15], (N_ATTN_LAYERS, HEAD_DIM), 0.02),
        'conv_w1': nrm(ks[16], (N_CONV_LAYERS, d, 2 * d), d ** -0.5),
        'conv_b1': nrm(ks[17], (N_CONV_LAYERS, 2 * d), 0.02),
        'conv_w_dw': nrm(ks[18], (N_CONV_LAYERS, CONV_WIDTH, d), CONV_WIDTH ** -0.5),
        'conv_b_dw': nrm(ks[19], (N_CONV_LAYERS, d), 0.02),
        'conv_ln_g': 1.0 + nrm(ks[20], (N_CONV_LAYERS, d), 0.02),
        'conv_ln_b': nrm(ks[21], (N_CONV_LAYERS, d), 0.02),
        'conv_w2': nrm(ks[22], (N_CONV_LAYERS, d, d), d ** -0.5),
        'conv_b2': nrm(ks[23], (N_CONV_LAYERS, d), 0.02),
    }


def reference(x, c, ctx, c_ctx, ada_w, ada_b, norm_g, ffn_w_gate, ffn_w_up, ffn_w_down,
              attn_wq, attn_wk, attn_wv, attn_wo, attn_q_g, attn_k_g,
              conv_w1, conv_b1, conv_w_dw, conv_b_dw, conv_ln_g, conv_ln_b, conv_w2, conv_b2):
    S = x.shape[1]
    ROWS = S // GRID_W
    cos, sin = axial_rope_tables(ROWS, x.dtype)
    h_lat, h_ctx = x, ctx
    silu_c = jax.nn.silu(c)
    silu_cc = jax.nn.silu(c_ctx)
    for i in range(DEPTH):
        kind = i % N_MIXERS
        j = i // N_MIXERS
        last = i == DEPTH - 1
        run_ctx = (not last) or kind == 0
        m = jnp.split((silu_c @ ada_w[i] + ada_b[i])[:, None, :], N_ADA, axis=-1)
        h_lat = ffn_half(h_lat, m[0], m[1], m[2], norm_g[i, 0],
                         ffn_w_gate[i, 0], ffn_w_up[i, 0], ffn_w_down[i, 0])
        a_lat = modulate(rms_norm(h_lat, norm_g[i, 1]), m[3], m[4])
        if run_ctx:
            mc = jnp.split(silu_cc @ ada_w[i] + ada_b[i], N_ADA, axis=-1)
            h_ctx = ffn_half(h_ctx, mc[0], mc[1], mc[2], norm_g[i, 0],
                             ffn_w_gate[i, 0], ffn_w_up[i, 0], ffn_w_down[i, 0])
            a_ctx = modulate(rms_norm(h_ctx, norm_g[i, 1]), mc[3], mc[4])
        if kind == 0:
            y_lat, y_ctx = attention_mixer(a_lat, a_ctx, attn_wq[j], attn_wk[j], attn_wv[j], attn_wo[j],
                                           attn_q_g[j], attn_k_g[j], cos, sin, not last)
        else:
            conv_args = (conv_w1[j], conv_b1[j], conv_w_dw[j], conv_b_dw[j],
                         conv_ln_g[j], conv_ln_b[j], conv_w2[j], conv_b2[j])
            y_lat = conformer_conv(a_lat, *conv_args)
            y_ctx = None if last else conformer_conv(a_ctx, *conv_args)
        h_lat = h_lat + m[5] * y_lat
        h_lat = ffn_half(h_lat, m[6], m[7], m[8], norm_g[i, 2],
                         ffn_w_gate[i, 1], ffn_w_up[i, 1], ffn_w_down[i, 1])
        if not last:
            h_ctx = h_ctx + mc[5] * y_ctx
            h_ctx = ffn_half(h_ctx, mc[6], mc[7], mc[8], norm_g[i, 2],
                             ffn_w_gate[i, 1], ffn_w_up[i, 1], ffn_w_down[i, 1])
    return h_lat
```

```python
import functools

import jax
import jax.numpy as jnp
from jax import lax
from jax.experimental import pallas as pl
from jax.experimental.pallas import tpu as pltpu

GRID_W = 64
ROPE_THETA = 10000.0
EPS = 1e-6
FFN_RESIDUAL = 0.5
N_ADA = 9
CONV_HALO = 16
VMEM_LIMIT_BYTES = 56 * 1024 * 1024

F32 = jnp.float32
BF16 = jnp.bfloat16


def _params(*sem):
    return pltpu.CompilerParams(dimension_semantics=sem, vmem_limit_bytes=VMEM_LIMIT_BYTES)


def _tile(n, pref):
    t = min(n, pref)
    while n % t:
        t -= 1
    return t


def _dot(a, b):
    return jnp.dot(a, b, preferred_element_type=F32)


class _Rows:
    def __init__(self, m, rows_per_group, group0):
        self.m, self.rpg, self.group0 = m, rows_per_group, group0

    def mod_spec(self, j, tm, d):
        per = self.rpg // tm
        g0 = self.group0

        def index_map(i, *_):
            return ((g0 + i // per) * N_ADA + j, 0, 0)

        return pl.BlockSpec((None, 1, d), index_map)


def _ada_kernel(c_ref, w_ref, b_ref, o_ref):
    c = c_ref[...]
    s = c * jax.nn.sigmoid(c)
    w = w_ref[...]
    s_hi = s.astype(BF16)
    s_lo = (s - s_hi.astype(F32)).astype(BF16)
    w_hi = w.astype(BF16)
    w_lo = (w - w_hi.astype(F32)).astype(BF16)
    acc = _dot(s_hi, w_hi) + _dot(s_lo, w_hi) + _dot(s_hi, w_lo)
    o_ref[...] = acc + b_ref[...]


def _ada(c_all, ada_w, ada_b):
    r, d = c_all.shape
    nl, _, n = ada_w.shape
    tn = _tile(n, 512)
    return pl.pallas_call(
        _ada_kernel,
        grid=(nl, n // tn),
        in_specs=[
            pl.BlockSpec((r, d), lambda l, j: (0, 0)),
            pl.BlockSpec((None, d, tn), lambda l, j: (l, 0, j)),
            pl.BlockSpec((None, 1, tn), lambda l, j: (l, 0, j)),
        ],
        out_specs=pl.BlockSpec((None, r, tn), lambda l, j: (l, 0, j)),
        out_shape=jax.ShapeDtypeStruct((nl, r, n), F32),
        compiler_params=_params("parallel", "parallel"),
        name="ada_mod",
    )(c_all, ada_w, ada_b.reshape(nl, 1, n))


def _norm_mod_math(x, g, shift, scale):
    r = lax.rsqrt(jnp.mean(x * x, axis=-1, keepdims=True) + EPS)
    return (x * r * g) * (1.0 + scale) + shift


def _norm_mod_kernel(x_ref, g_ref, sh_ref, sc_ref, o_ref):
    o_ref[...] = _norm_mod_math(x_ref[...], g_ref[...], sh_ref[...], sc_ref[...]).astype(o_ref.dtype)


def _norm_mod(x, g, mods, rows, j_shift):
    m, d = x.shape
    tm = _tile(rows.rpg, 256)
    return pl.pallas_call(
        _norm_mod_kernel,
        grid=(m // tm,),
        in_specs=[
            pl.BlockSpec((tm, d), lambda i: (i, 0)),
            pl.BlockSpec((1, d), lambda i: (0, 0)),
            rows.mod_spec(j_shift, tm, d),
            rows.mod_spec(j_shift + 1, tm, d),
        ],
        out_specs=pl.BlockSpec((tm, d), lambda i: (i, 0)),
        out_shape=jax.ShapeDtypeStruct((m, d), BF16),
        compiler_params=_params("parallel"),
        name="norm_mod",
    )(x, g.reshape(1, d), mods, mods)


def _swiglu_kernel(h_ref, wg_ref, wu_ref, o_ref):
    h = h_ref[...]
    g = _dot(h, wg_ref[...])
    u = _dot(h, wu_ref[...])
    o_ref[...] = ((g * jax.nn.sigmoid(g)) * u).astype(o_ref.dtype)


def _glu_kernel(h_ref, wa_ref, wg_ref, ba_ref, bg_ref, o_ref):
    h = h_ref[...]
    a = _dot(h, wa_ref[...]) + ba_ref[...]
    g = _dot(h, wg_ref[...]) + bg_ref[...]
    o_ref[...] = (a * jax.nn.sigmoid(g)).astype(o_ref.dtype)


def _swiglu(h, wg, wu):
    m, d = h.shape
    f = wg.shape[1]
    tm, tn = _tile(m, 1024), _tile(f, 512)
    return pl.pallas_call(
        _swiglu_kernel,
        grid=(m // tm, f // tn),
        in_specs=[
            pl.BlockSpec((tm, d), lambda i, n: (i, 0)),
            pl.BlockSpec((d, tn), lambda i, n: (0, n)),
            pl.BlockSpec((d, tn), lambda i, n: (0, n)),
        ],
        out_specs=pl.BlockSpec((tm, tn), lambda i, n: (i, n)),
        out_shape=jax.ShapeDtypeStruct((m, f), BF16),
        compiler_params=_params("parallel", "arbitrary"),
        name="ffn_swiglu",
    )(h, wg, wu)


def _glu(h, w1, b1, out_dtype):
    m, d = h.shape
    n = w1.shape[1] // 2
    tm, tn = _tile(m, 1024), _tile(n, 512)
    off = n // tn
    b1 = b1.reshape(1, 2 * n)
    return pl.pallas_call(
        _glu_kernel,
        grid=(m // tm, n // tn),
        in_specs=[
            pl.BlockSpec((tm, d), lambda i, j: (i, 0)),
            pl.BlockSpec((d, tn), lambda i, j: (0, j)),
            pl.BlockSpec((d, tn), lambda i, j: (0, j + off)),
            pl.BlockSpec((1, tn), lambda i, j: (0, j)),
            pl.BlockSpec((1, tn), lambda i, j: (0, j + off)),
        ],
        out_specs=pl.BlockSpec((tm, tn), lambda i, j: (i, j)),
        out_shape=jax.ShapeDtypeStruct((m, n), out_dtype),
        compiler_params=_params("parallel", "arbitrary"),
        name="conv_glu",
    )(h, w1, w1, b1, b1)


def _mm_resid_kernel(*refs, coef, nk, has_bias, fuse_norm):
    a_ref, w_ref, x_ref, gate_ref = refs[:4]
    pos = 4
    bias_ref = None
    if has_bias:
        bias_ref = refs[pos]
        pos += 1
    if fuse_norm:
        g_ref, sh_ref, sc_ref = refs[pos:pos + 3]
        pos += 3
        o_ref, h_ref = refs[pos:pos + 2]
    else:
        o_ref = refs[pos]
    k = pl.program_id(2)

    @pl.when(k == 0)
    def _():
        o_ref[...] = jnp.zeros_like(o_ref)

    o_ref[...] += _dot(a_ref[...], w_ref[...])

    @pl.when(k == nk - 1)
    def _():
        y = o_ref[...]
        if has_bias:
            y = y + bias_ref[...]
        xn = x_ref[...] + (coef * gate_ref[...]) * y
        o_ref[...] = xn
        if fuse_norm:
            h_ref[...] = _norm_mod_math(xn, g_ref[...], sh_ref[...], sc_ref[...]).astype(h_ref.dtype)


def _mm_resid(a, w, x, mods, rows, j_gate, coef, bias=None, norm=None, tm=1024, tn=1024, tk=2560):
    m, kk = a.shape
    n = w.shape[1]
    fuse_norm = norm is not None
    tm = _tile(rows.rpg, tm)
    tn = n if fuse_norm else _tile(n, tn)
    tk = _tile(kk, tk)
    nk = kk // tk
    in_specs = [
        pl.BlockSpec((tm, tk), lambda i, j, k: (i, k)),
        pl.BlockSpec((tk, tn), lambda i, j, k: (k, j)),
        pl.BlockSpec((tm, tn), lambda i, j, k: (i, j)),
        _col_mod_spec(rows, j_gate, tm, tn),
    ]
    args = [a, w, x, mods]
    if bias is not None:
        in_specs.append(pl.BlockSpec((1, tn), lambda i, j, k: (0, j)))
        args.append(bias.reshape(1, n))
    out_specs = pl.BlockSpec((tm, tn), lambda i, j, k: (i, j))
    out_shape = jax.ShapeDtypeStruct((m, n), F32)
    if fuse_norm:
        g, mods_n, j_shift = norm
        in_specs += [
            pl.BlockSpec((1, n), lambda i, j, k: (0, 0)),
            rows.mod_spec(j_shift, tm, n),
            rows.mod_spec(j_shift + 1, tm, n),
        ]
        args += [g.reshape(1, n), mods_n, mods_n]
        out_specs = [out_specs, pl.BlockSpec((tm, tn), lambda i, j, k: (i, j))]
        out_shape = [out_shape, jax.ShapeDtypeStruct((m, n), BF16)]
    return pl.pallas_call(
        functools.partial(_mm_resid_kernel, coef=coef, nk=nk, has_bias=bias is not None,
                          fuse_norm=fuse_norm),
        grid=(m // tm, n // tn, nk),
        in_specs=in_specs,
        out_specs=out_specs,
        out_shape=out_shape,
        compiler_params=_params("parallel", "parallel", "arbitrary"),
        name="mm_resid_norm" if fuse_norm else "mm_resid",
    )(*args)


def _resid_norm(a, w, x, mods, rows, j_gate, coef, bias=None, norm=None):
    x_new = _mm_resid(a, w, x, mods, rows, j_gate, coef, bias=bias)
    if norm is None:
        return x_new, None
    g, mods_n, j_shift = norm
    return x_new, _norm_mod(x_new, g, mods_n, rows, j_shift)


def _col_mod_spec(rows, j, tm, tn):
    per = rows.rpg // tm
    g0 = rows.group0
    return pl.BlockSpec((None, 1, tn), lambda i, n, k: ((g0 + i // per) * N_ADA + j, 0, n))


def _proj_kernel(*refs, hd, norm, rope, scale):
    a_ref, w_ref = refs[:2]
    pos = 2
    if norm:
        g_ref = refs[pos]
        pos += 1
    if rope:
        cos_ref, sa_ref, sb_ref = refs[pos:pos + 3]
        pos += 3
    o_ref = refs[pos]
    y = _dot(a_ref[...], w_ref[...])
    for h in range(y.shape[1] // hd):
        yh = y[:, h * hd:(h + 1) * hd]
        if norm:
            r = lax.rsqrt(jnp.mean(yh * yh, axis=-1, keepdims=True) + EPS)
            yh = yh * r * g_ref[...]
        if rope:
            yh = (yh * cos_ref[...] + pltpu.roll(yh, hd - hd // 4, 1) * sa_ref[...]
                  + pltpu.roll(yh, hd // 4, 1) * sb_ref[...])
        if scale != 1.0:
            yh = yh * scale
        o_ref[:, h * hd:(h + 1) * hd] = yh.astype(o_ref.dtype)


def _proj(a, w, hd, seq, g=None, rope=None, scale=1.0):
    m, d = a.shape
    n = w.shape[1]
    tm, tn = _tile(seq, 1024), _tile(n, 512)
    per = seq // tm
    in_specs = [
        pl.BlockSpec((tm, d), lambda i, j: (i, 0)),
        pl.BlockSpec((d, tn), lambda i, j: (0, j)),
    ]
    args = [a, w]
    if g is not None:
        in_specs.append(pl.BlockSpec((1, hd), lambda i, j: (0, 0)))
        args.append(g.reshape(1, hd))
    if rope is not None:
        in_specs += [pl.BlockSpec((tm, hd), lambda i, j: (i % per, 0))] * 3
        args += list(rope)
    return pl.pallas_call(
        functools.partial(_proj_kernel, hd=hd, norm=g is not None, rope=rope is not None, scale=scale),
        grid=(m // tm, n // tn),
        in_specs=in_specs,
        out_specs=pl.BlockSpec((tm, tn), lambda i, j: (i, j)),
        out_shape=jax.ShapeDtypeStruct((m, n), BF16),
        compiler_params=_params("parallel", "arbitrary"),
        name="head_proj",
    )(*args)


def _flash_kernel(q_ref, k_ref, v_ref, o_ref, m_ref, l_ref, acc_ref, *, groups, hd, tk, nchunks):
    tq = q_ref.shape[0]
    q = jnp.concatenate([q_ref[:, g * hd:(g + 1) * hd] for g in range(groups)], axis=0)
    m_ref[...] = jnp.full_like(m_ref, -1e30)
    l_ref[...] = jnp.zeros_like(l_ref)
    acc_ref[...] = jnp.zeros_like(acc_ref)

    def body(c, carry):
        start = pl.multiple_of(c * tk, tk)
        kc = k_ref[pl.ds(start, tk), :]
        vc = v_ref[pl.ds(start, tk), :]
        s = lax.dot_general(kc, q, (((1,), (1,)), ((), ())), preferred_element_type=F32)
        m_old = m_ref[...]
        m_new = jnp.maximum(m_old, jnp.max(s, axis=0, keepdims=True))
        alpha = jnp.exp(m_old - m_new)
        p = jnp.exp(s - m_new)
        l_ref[...] = alpha * l_ref[...] + jnp.sum(p, axis=0, keepdims=True)
        pv = lax.dot_general(vc, p.astype(BF16), (((0,), (0,)), ((), ())),
                             preferred_element_type=F32)
        acc_ref[...] = alpha * acc_ref[...] + pv
        m_ref[...] = m_new
        return carry

    lax.fori_loop(0, nchunks, body, 0)
    out = (acc_ref[...] / l_ref[...]).T
    for g in range(groups):
        o_ref[:, g * hd:(g + 1) * hd] = out[g * tq:(g + 1) * tq, :].astype(o_ref.dtype)


def _flash(q, k, v, hd):
    b, sq, hq = q.shape
    t, hkv = k.shape[1], k.shape[2]
    nkv = hkv // hd
    groups = hq // hkv
    tq = _tile(sq, 256)
    tk = _tile(t, 768)
    gw = groups * hd
    return pl.pallas_call(
        functools.partial(_flash_kernel, groups=groups, hd=hd, tk=tk, nchunks=t // tk),
        grid=(b, nkv, sq // tq),
        in_specs=[
            pl.BlockSpec((None, tq, gw), lambda bi, h, i: (bi, i, h)),
            pl.BlockSpec((None, t, hd), lambda bi, h, i: (bi, 0, h)),
            pl.BlockSpec((None, t, hd), lambda bi, h, i: (bi, 0, h)),
        ],
        out_specs=pl.BlockSpec((None, tq, gw), lambda bi, h, i: (bi, i, h)),
        out_shape=jax.ShapeDtypeStruct((b, sq, hq), BF16),
        scratch_shapes=[
            pltpu.VMEM((1, groups * tq), F32),
            pltpu.VMEM((1, groups * tq), F32),
            pltpu.VMEM((hd, groups * tq), F32),
        ],
        compiler_params=_params("parallel", "parallel", "arbitrary"),
        name="flash_attn",
    )(q, k, v)


def _dwconv_kernel(prev_ref, cur_ref, next_ref, w_ref, b_ref, g_ref, beta_ref, o_ref, win_ref, y_ref,
                   *, width, nblk, rc, cc):
    ts, d = cur_ref.shape
    j = pl.program_id(1)
    pad = width // 2
    win_ref[0:CONV_HALO, :] = jnp.where(j > 0, prev_ref[...], 0.0)
    win_ref[CONV_HALO:CONV_HALO + ts, :] = cur_ref[...]
    win_ref[CONV_HALO + ts:, :] = jnp.where(j < nblk - 1, next_ref[...], 0.0)
    for r0 in range(0, ts, rc):
        for c0 in range(0, d, cc):
            acc = jnp.zeros((rc, cc), F32)
            for k in range(width):
                off = CONV_HALO - pad + k + r0
                acc = acc + win_ref[off:off + rc, c0:c0 + cc] * w_ref[k:k + 1, c0:c0 + cc]
            y_ref[r0:r0 + rc, c0:c0 + cc] = acc
    y = y_ref[...] + b_ref[...]
    mu = jnp.mean(y, axis=-1, keepdims=True)
    yc = y - mu
    z = yc * lax.rsqrt(jnp.mean(yc * yc, axis=-1, keepdims=True) + EPS) * g_ref[...] + beta_ref[...]
    o_ref[...] = (z * jax.nn.sigmoid(z)).astype(o_ref.dtype)


def _dwconv_ln_silu(u, w_dw, b_dw, ln_g, ln_b):
    b, s, d = u.shape
    width = w_dw.shape[0]
    ts = _tile(s, 64)
    hb = ts // CONV_HALO
    nblk = s // ts
    last_h = s // CONV_HALO - 1
    vec = lambda a: a.reshape(1, d)
    row = pl.BlockSpec((1, d), lambda bi, j: (0, 0))
    return pl.pallas_call(
        functools.partial(_dwconv_kernel, width=width, nblk=nblk, rc=_tile(ts, 32), cc=_tile(d, 1024)),
        grid=(b, nblk),
        in_specs=[
            pl.BlockSpec((None, CONV_HALO, d), lambda bi, j: (bi, jnp.maximum(j * hb - 1, 0), 0)),
            pl.BlockSpec((None, ts, d), lambda bi, j: (bi, j, 0)),
            pl.BlockSpec((None, CONV_HALO, d), lambda bi, j: (bi, jnp.minimum((j + 1) * hb, last_h), 0)),
            pl.BlockSpec((width, d), lambda bi, j: (0, 0)),
            row, row, row,
        ],
        out_specs=pl.BlockSpec((None, ts, d), lambda bi, j: (bi, j, 0)),
        out_shape=jax.ShapeDtypeStruct((b, s, d), BF16),
        scratch_shapes=[pltpu.VMEM((ts + 2 * CONV_HALO, d), F32), pltpu.VMEM((ts, d), F32)],
        compiler_params=_params("parallel", "arbitrary"),
        name="dwconv_ln_silu",
    )(u, u, u, w_dw, vec(b_dw), vec(ln_g), vec(ln_b))


def _rope_tables(seq, hd):
    half = hd // 2
    t = jnp.arange(seq)
    row = (t // GRID_W).astype(F32)
    col = (t % GRID_W).astype(F32)
    inv_freq = 1.0 / (ROPE_THETA ** (jnp.arange(0, half, 2, dtype=F32) / half))
    ang_r = row[:, None] * inv_freq[None, :]
    ang_c = col[:, None] * inv_freq[None, :]
    ang = jnp.concatenate([ang_r, ang_r, ang_c, ang_c], axis=-1)
    cos, sin = jnp.cos(ang), jnp.sin(ang)
    first = (jnp.arange(hd) % half) < (half // 2)
    sin_a = jnp.where(first[None, :], -sin, 0.0)
    sin_b = jnp.where(first[None, :], 0.0, sin)
    return cos, sin_a, sin_b


def _ffn_half(x, h, mods, rows, j0, wg, wu, wd, next_norm):
    act = _swiglu(h, wg, wu)
    return _resid_norm(act, wd, x, mods, rows, j0 + 2, FFN_RESIDUAL, norm=next_norm)


def kernel(x, c, ctx, c_ctx, ada_w, ada_b, norm_g, ffn_w_gate, ffn_w_up, ffn_w_down, attn_wq, attn_wk,
           attn_wv, attn_wo, attn_q_g, attn_k_g, conv_w1, conv_b1, conv_w_dw, conv_b_dw, conv_ln_g,
           conv_ln_b, conv_w2, conv_b2):
    B, S, D = x.shape
    C = ctx.shape[1]
    depth = ada_w.shape[0]
    hd = attn_q_g.shape[-1]
    n_mixers = 2

    wg_all, wu_all, wd_all = (w.astype(BF16) for w in (ffn_w_gate, ffn_w_up, ffn_w_down))
    wq_all, wk_all, wv_all, wo_all = (w.astype(BF16) for w in (attn_wq, attn_wk, attn_wv, attn_wo))
    w1_all, w2_all = conv_w1.astype(BF16), conv_w2.astype(BF16)

    n_rows = B + 1
    pad = (-n_rows) % 8
    c_all = jnp.concatenate([c, c_ctx[None, :], jnp.zeros((pad, D), F32)], axis=0)
    mods_all = _ada(c_all, ada_w, ada_b)
    mods_of = lambda i: mods_all[i, :n_rows].reshape(n_rows * N_ADA, 1, D)

    lat = _Rows(B * S, S, 0)
    cx = _Rows(B * C, B * C, B)
    rope = _rope_tables(S, hd)
    attn_scale = float(hd) ** -0.5

    h_lat = x.reshape(B * S, D)
    h_ctx = ctx.reshape(B * C, D)
    hn_lat = _norm_mod(h_lat, norm_g[0, 0], mods_of(0), lat, 0)
    hn_ctx = _norm_mod(h_ctx, norm_g[0, 0], mods_of(0), cx, 0)
    for i in range(depth):
        kind, j = i % n_mixers, i // n_mixers
        last = i == depth - 1
        run_ctx = (not last) or kind == 0
        mods = mods_of(i)
        g = norm_g[i]

        def ffn(s, x_, h_, rows_, next_norm):
            return _ffn_half(x_, h_, mods, rows_, 6 * s, wg_all[i, s], wu_all[i, s], wd_all[i, s], next_norm)

        h_lat, a_lat = ffn(0, h_lat, hn_lat, lat, (g[1], mods, 3))
        if run_ctx:
            h_ctx, a_ctx = ffn(0, h_ctx, hn_ctx, cx, (g[1], mods, 3))

        post = (g[2], mods, 6)
        if kind == 0:
            q = _proj(a_lat, wq_all[j], hd, S, g=attn_q_g[j], rope=rope, scale=attn_scale)
            k = _proj(a_lat, wk_all[j], hd, S, g=attn_k_g[j], rope=rope)
            v = _proj(a_lat, wv_all[j], hd, S)
            kc = _proj(a_ctx, wk_all[j], hd, C, g=attn_k_g[j])
            vc = _proj(a_ctx, wv_all[j], hd, C)
            nkv = k.shape[1]
            k_all = jnp.concatenate([k.reshape(B, S, nkv), kc.reshape(B, C, nkv)], axis=1)
            v_all = jnp.concatenate([v.reshape(B, S, nkv), vc.reshape(B, C, nkv)], axis=1)
            o = _flash(q.reshape(B, S, -1), k_all, v_all, hd).reshape(B * S, -1)
            h_lat, hn_lat = _resid_norm(o, wo_all[j], h_lat, mods, lat, 5, 1.0, norm=post)
            if not last:
                qc = _proj(a_ctx, wq_all[j], hd, C, g=attn_q_g[j], scale=attn_scale)
                oc = _flash(qc.reshape(B, C, -1), kc.reshape(B, C, nkv), vc.reshape(B, C, nkv), hd)
                h_ctx, hn_ctx = _resid_norm(oc.reshape(B * C, -1), wo_all[j], h_ctx, mods, cx, 5, 1.0,
                                            norm=post)
        else:
            def conv_mixer(a, n_seq, seq):
                u = _glu(a, w1_all[j], conv_b1[j], F32)
                return _dwconv_ln_silu(u.reshape(n_seq, seq, D), conv_w_dw[j], conv_b_dw[j],
                                       conv_ln_g[j], conv_ln_b[j]).reshape(n_seq * seq, D)

            h_lat, hn_lat = _resid_norm(conv_mixer(a_lat, B, S), w2_all[j], h_lat, mods, lat, 5, 1.0,
                                        bias=conv_b2[j], norm=post)
            if not last:
                h_ctx, hn_ctx = _resid_norm(conv_mixer(a_ctx, B, C), w2_all[j], h_ctx, mods, cx, 5, 1.0,
                                            bias=conv_b2[j], norm=post)

        nxt = None if last else (norm_g[i + 1, 0], mods_of(i + 1), 0)
        h_lat, hn_lat = ffn(1, h_lat, hn_lat, lat, nxt)
        if not last:
            h_ctx, hn_ctx = ffn(1, h_ctx, hn_ctx, cx, nxt)
    return h_lat.reshape(B, S, D)
```

```python
import functools
import math

import jax
import jax.numpy as jnp
from jax import lax
from jax.experimental import pallas as pl
from jax.experimental.pallas import tpu as pltpu

GRID_W = 64
ROPE_THETA = 10000.0
EPS = 1e-6
FFN_RESIDUAL = 0.5
N_ADA = 9
CONV_HALO = 16
PROJ_TN = 512
SUBLANES = 8
VMEM_LIMIT_BYTES = 56 * 1024 * 1024

F32 = jnp.float32
BF16 = jnp.bfloat16


def _params(*sem):
    return pltpu.CompilerParams(dimension_semantics=sem, vmem_limit_bytes=VMEM_LIMIT_BYTES)


def _tile(n, pref):
    t = min(n, pref)
    while n % t:
        t -= 1
    return t


def _dot(a, b):
    return jnp.dot(a, b, preferred_element_type=F32)


class _Rows:
    def __init__(self, m, rows_per_group, group0):
        self.m, self.rpg, self.group0 = m, rows_per_group, group0

    def mod_spec(self, j, tm, d):
        per = self.rpg // tm
        g0 = self.group0

        def index_map(i, *_):
            return ((g0 + i // per) * N_ADA + j, 0, 0)

        return pl.BlockSpec((None, 1, d), index_map)


def _ada_kernel(c_ref, w_ref, b_ref, o_ref):
    c = c_ref[...]
    s = c * jax.nn.sigmoid(c)
    w = w_ref[...]
    s_hi = s.astype(BF16)
    s_lo = (s - s_hi.astype(F32)).astype(BF16)
    w_hi = w.astype(BF16)
    w_lo = (w - w_hi.astype(F32)).astype(BF16)
    acc = _dot(s_hi, w_hi) + _dot(s_lo, w_hi) + _dot(s_hi, w_lo)
    o_ref[...] = acc + b_ref[...]


def _ada(c_all, ada_w, ada_b):
    r, d = c_all.shape
    nl, _, n = ada_w.shape
    tn = _tile(n, 512)
    return pl.pallas_call(
        _ada_kernel,
        grid=(nl, n // tn),
        in_specs=[
            pl.BlockSpec((r, d), lambda l, j: (0, 0)),
            pl.BlockSpec((None, d, tn), lambda l, j: (l, 0, j)),
            pl.BlockSpec((None, 1, tn), lambda l, j: (l, 0, j)),
        ],
        out_specs=pl.BlockSpec((None, r, tn), lambda l, j: (l, 0, j)),
        out_shape=jax.ShapeDtypeStruct((nl, r, n), F32),
        compiler_params=_params("parallel", "parallel"),
        name="ada_mod",
    )(c_all, ada_w, ada_b.reshape(nl, 1, n))


def _norm_mod_math(x, g, shift, scale):
    r = lax.rsqrt(jnp.mean(x * x, axis=-1, keepdims=True) + EPS)
    return (x * r * g) * (1.0 + scale) + shift


def _norm_mod_kernel(x_ref, g_ref, sh_ref, sc_ref, o_ref):
    o_ref[...] = _norm_mod_math(x_ref[...], g_ref[...], sh_ref[...], sc_ref[...]).astype(o_ref.dtype)


def _norm_mod(x, g, mods, rows, j_shift):
    m, d = x.shape
    tm = _tile(rows.rpg, 256)
    return pl.pallas_call(
        _norm_mod_kernel,
        grid=(m // tm,),
        in_specs=[
            pl.BlockSpec((tm, d), lambda i: (i, 0)),
            pl.BlockSpec((1, d), lambda i: (0, 0)),
            rows.mod_spec(j_shift, tm, d),
            rows.mod_spec(j_shift + 1, tm, d),
        ],
        out_specs=pl.BlockSpec((tm, d), lambda i: (i, 0)),
        out_shape=jax.ShapeDtypeStruct((m, d), BF16),
        compiler_params=_params("parallel"),
        name="norm_mod",
    )(x, g.reshape(1, d), mods, mods)


def _swiglu_kernel(h_ref, wg_ref, wu_ref, o_ref):
    h = h_ref[...]
    g = _dot(h, wg_ref[...])
    u = _dot(h, wu_ref[...])
    o_ref[...] = ((g * jax.nn.sigmoid(g)) * u).astype(o_ref.dtype)


def _glu_kernel(h_ref, wa_ref, wg_ref, ba_ref, bg_ref, o_ref):
    h = h_ref[...]
    a = _dot(h, wa_ref[...]) + ba_ref[...]
    g = _dot(h, wg_ref[...]) + bg_ref[...]
    o_ref[...] = (a * jax.nn.sigmoid(g)).astype(o_ref.dtype)


def _stacked(w):
    arr, lead = w if isinstance(w, tuple) else (w, ())
    return arr, tuple(lead), arr.shape[len(lead):]


def _w_spec(lead, block, index_map):
    return pl.BlockSpec((None,) * len(lead) + block, lambda *g: lead + index_map(*g))


def _swiglu_cast_kernel(h_ref, wg_ref, wu_ref, c0_ref, c1_ref, c2_ref, o_ref, o0_ref, o1_ref, o2_ref):
    _swiglu_kernel(h_ref, wg_ref, wu_ref, o_ref)
    o0_ref[...] = c0_ref[...].astype(o0_ref.dtype)
    o1_ref[...] = c1_ref[...].astype(o1_ref.dtype)
    o2_ref[...] = c2_ref[...].astype(o2_ref.dtype)


def _swiglu(h, wg, wu, cast=None):
    m, d = h.shape
    wg, lead_g, (_, f) = _stacked(wg)
    wu, lead_u, _ = _stacked(wu)
    tm, tn = _tile(m, 1024), _tile(f, 512)
    gm, gn = m // tm, f // tn
    in_specs = [
        pl.BlockSpec((tm, d), lambda i, n: (i, 0)),
        _w_spec(lead_g, (d, tn), lambda i, n: (0, n)),
        _w_spec(lead_u, (d, tn), lambda i, n: (0, n)),
    ]
    out_specs = pl.BlockSpec((tm, tn), lambda i, n: (i, n))
    out_shape = jax.ShapeDtypeStruct((m, f), BF16)
    args = [h, wg, wu]
    body = _swiglu_kernel
    if cast is not None:
        body = _swiglu_cast_kernel
        out_specs, out_shape = [out_specs], [out_shape]
        for which, w in enumerate(cast):
            arr, lead, (r, c) = _stacked(w)
            blk, imap = ((r // gm, c // gn), lambda i, n: (i, n)) if which < 2 else \
                        ((r // gn, c // gm), lambda i, n: (n, i))
            assert r % (blk[0]) == 0 and c % blk[1] == 0 and blk[0] % 16 == 0 and blk[1] % 128 == 0
            in_specs.append(_w_spec(lead, blk, imap))
            out_specs.append(pl.BlockSpec(blk, imap))
            out_shape.append(jax.ShapeDtypeStruct((r, c), BF16))
            args.append(arr)
    return pl.pallas_call(
        body,
        grid=(gm, gn),
        in_specs=in_specs,
        out_specs=out_specs,
        out_shape=out_shape,
        compiler_params=_params("parallel", "arbitrary"),
        name="ffn_swiglu_cast" if cast is not None else "ffn_swiglu",
    )(*args)


def _glu(h, w1, b1, out_dtype):
    m, d = h.shape
    n = w1.shape[1] // 2
    tm, tn = _tile(m, 1024), _tile(n, 512)
    off = n // tn
    b1 = b1.reshape(1, 2 * n)
    return pl.pallas_call(
        _glu_kernel,
        grid=(m // tm, n // tn),
        in_specs=[
            pl.BlockSpec((tm, d), lambda i, j: (i, 0)),
            pl.BlockSpec((d, tn), lambda i, j: (0, j)),
            pl.BlockSpec((d, tn), lambda i, j: (0, j + off)),
            pl.BlockSpec((1, tn), lambda i, j: (0, j)),
            pl.BlockSpec((1, tn), lambda i, j: (0, j + off)),
        ],
        out_specs=pl.BlockSpec((tm, tn), lambda i, j: (i, j)),
        out_shape=jax.ShapeDtypeStruct((m, n), out_dtype),
        compiler_params=_params("parallel", "arbitrary"),
        name="conv_glu",
    )(h, w1, w1, b1, b1)


def _mm_resid_kernel(*refs, coef, nk, has_bias):
    a_ref, w_ref, x_ref, gate_ref = refs[:4]
    bias_ref = refs[4] if has_bias else None
    o_ref = refs[-1]
    k = pl.program_id(2)

    @pl.when(k == 0)
    def _():
        o_ref[...] = jnp.zeros_like(o_ref)

    o_ref[...] += _dot(a_ref[...], w_ref[...])

    @pl.when(k == nk - 1)
    def _():
        y = o_ref[...]
        if has_bias:
            y = y + bias_ref[...]
        o_ref[...] = x_ref[...] + (coef * gate_ref[...]) * y


def _mm_resid(a, w, x, mods, rows, j_gate, coef, bias=None):
    m, kk = a.shape
    w, lead, (_, n) = _stacked(w)
    tm, tn, tk = _tile(rows.rpg, 1024), _tile(n, 1024), _tile(kk, 2560)
    nk = kk // tk
    in_specs = [
        pl.BlockSpec((tm, tk), lambda i, j, k: (i, k)),
        _w_spec(lead, (tk, tn), lambda i, j, k: (k, j)),
        pl.BlockSpec((tm, tn), lambda i, j, k: (i, j)),
        _col_mod_spec(rows, j_gate, tm, tn),
    ]
    args = [a, w, x, mods]
    if bias is not None:
        in_specs.append(pl.BlockSpec((1, tn), lambda i, j, k: (0, j)))
        args.append(bias.reshape(1, n))
    return pl.pallas_call(
        functools.partial(_mm_resid_kernel, coef=coef, nk=nk, has_bias=bias is not None),
        grid=(m // tm, n // tn, nk),
        in_specs=in_specs,
        out_specs=pl.BlockSpec((tm, tn), lambda i, j, k: (i, j)),
        out_shape=jax.ShapeDtypeStruct((m, n), F32),
        compiler_params=_params("parallel", "parallel", "arbitrary"),
        name="mm_resid",
    )(*args)


def _resid_norm(a, w, x, mods, rows, j_gate, coef, bias=None, norm=None):
    x_new = _mm_resid(a, w, x, mods, rows, j_gate, coef, bias=bias)
    if norm is None:
        return x_new, None
    g, mods_n, j_shift = norm
    return x_new, _norm_mod(x_new, g, mods_n, rows, j_shift)


def _col_mod_spec(rows, j, tm, tn):
    per = rows.rpg // tm
    g0 = rows.group0
    return pl.BlockSpec((None, 1, tn), lambda i, n, k: ((g0 + i // per) * N_ADA + j, 0, n))


def _proj_kernel(*refs, hd, norm, rope, scale):
    a_ref, w_ref = refs[:2]
    pos = 2
    if norm:
        g_ref, ones_ref = refs[pos:pos + 2]
        pos += 2
    if rope:
        cos_ref, sin_ref, perm_ref = refs[pos:pos + 3]
        pos += 3
    o_ref = refs[pos]
    y = _dot(a_ref[...], w_ref[...])
    if norm:
        ss = _dot((y * y).astype(BF16), ones_ref[...])
        y = y * lax.rsqrt(ss * (1.0 / hd) + EPS) * g_ref[...]
    if rope:
        y = y * cos_ref[...] + _dot(y.astype(BF16), perm_ref[...]) * sin_ref[...]
    if scale != 1.0:
        y = y * scale
    o_ref[...] = y.astype(o_ref.dtype)


def _head_consts(hd, heads):
    n = heads * hd
    i = jnp.arange(n)
    same_head = (i[:, None] // hd) == (i[None, :] // hd)
    ones = same_head.astype(BF16)
    first = (i % (hd // 2)) < (hd // 4)
    src = jnp.where(first, i + hd // 4, i - hd // 4)
    sign = jnp.where(first, -1.0, 1.0)
    perm = jnp.where(i[:, None] == src[None, :], sign[None, :], 0.0).astype(BF16)
    return ones, perm


def _proj(a, w, hd, seq, g=None, rope=None, scale=1.0):
    m, d = a.shape
    n = w.shape[1]
    tm, tn = _tile(seq, 1024), _tile(n, PROJ_TN)
    per = seq // tm
    heads = tn // hd
    ones, perm = _head_consts(hd, heads)
    const = lambda shape: pl.BlockSpec(shape, lambda i, j: (0, 0))
    in_specs = [
        pl.BlockSpec((tm, d), lambda i, j: (i, 0)),
        pl.BlockSpec((d, tn), lambda i, j: (0, j)),
    ]
    args = [a, w]
    if g is not None:
        in_specs += [const((1, tn)), const((tn, tn))]
        args += [jnp.tile(g.reshape(1, hd), (1, heads)), ones]
    if rope is not None:
        in_specs += [pl.BlockSpec((tm, tn), lambda i, j: (i % per, 0))] * 2 + [const((tn, tn))]
        args += [jnp.tile(rope[0], (1, heads)), jnp.tile(rope[1], (1, heads)), perm]
    return pl.pallas_call(
        functools.partial(_proj_kernel, hd=hd, norm=g is not None, rope=rope is not None, scale=scale),
        grid=(m // tm, n // tn),
        in_specs=in_specs,
        out_specs=pl.BlockSpec((tm, tn), lambda i, j: (i, j)),
        out_shape=jax.ShapeDtypeStruct((m, n), BF16),
        compiler_params=_params("parallel", "arbitrary"),
        name="head_proj",
    )(*args)


def _zero_after(x):
    bits = pltpu.bitcast(x, jnp.uint32)
    slabs = [bits[i:i + SUBLANES] for i in range(0, bits.shape[0], SUBLANES)]
    acc = functools.reduce(jnp.bitwise_or, slabs)
    acc = functools.reduce(jnp.bitwise_or, [acc[:, j:j + 128] for j in range(0, acc.shape[1], 128)])
    zero = jnp.right_shift(jnp.right_shift(acc, jnp.uint32(16)), jnp.uint32(16))
    return pltpu.bitcast(zero, BF16)


def _flash_kernel(q_ref, k_ref, vt_ref, o_ref, q_scr, s0_ref, s1_ref, p0_ref, p1_ref, mx0_ref, mx1_ref,
                  al0_ref, al1_ref, m_ref, acc_ref, *, groups, hd, tk, rb, nchunks):
    tq = q_ref.shape[0]
    for g in range(groups):
        q_scr[g * tq:(g + 1) * tq, :] = q_ref[:, g * hd:(g + 1) * hd]
    m_ref[...] = jnp.full_like(m_ref, -1e30)
    acc_ref[...] = jnp.zeros_like(acc_ref)
    s_bufs, p_bufs = (s0_ref, s1_ref), (p0_ref, p1_ref)
    mx_bufs, al_bufs = (mx0_ref, mx1_ref), (al0_ref, al1_ref)

    nsub = tk // rb

    def step(c, par, do_scores, do_probs, do_acc):
        if do_probs:
            m_old = m_ref[...]
            m_new = jnp.maximum(m_old, mx_bufs[1 - par][...])
            al_bufs[1 - par][...] = jnp.exp2(m_old - m_new)
            m_ref[...] = m_new
        if do_scores:
            start = (c + 2) * tk
            if not isinstance(start, int):
                start = pl.multiple_of(start, tk)
            mx = None
        edge = None
        for r in range(nsub):
            rows = slice(r * rb, (r + 1) * rb)
            if do_scores:
                k_rows = k_ref[pl.ds(start + r * rb, rb), :]
                if edge is not None:
                    k_rows = jnp.concatenate([k_rows[:edge.shape[0]] + edge, k_rows[edge.shape[0]:]], axis=0)
                s = lax.dot_general(k_rows, q_scr[...], (((1,), (1,)), ((), ())),
                                    preferred_element_type=F32)
                s_bufs[par][rows, :] = s
                smx = jnp.max(s, axis=0, keepdims=True)
                mx = smx if mx is None else jnp.maximum(mx, smx)
            if do_probs:
                p = jnp.exp2((s_bufs[1 - par][rows, :] - m_new).astype(BF16))
                p_bufs[1 - par][rows, :] = p
                edge = _zero_after(p)
            if do_acc:
                vt = vt_ref[c, :, rows]
                if edge is not None and r == nsub - 1:
                    er, lanes = edge.shape[0], vt.shape[1]
                    top = vt[:er] + jnp.concatenate([edge] * (lanes // edge.shape[1]), axis=1)
                    vt = jnp.concatenate([top, vt[er:]], axis=0)
                pv = _dot(vt, p_bufs[par][rows, :])
                acc_ref[...] = (al_bufs[par][...] * acc_ref[...] if r == 0 else acc_ref[...]) + pv
        if do_scores:
            mx_bufs[par][...] = mx

    step(-2, 0, True, False, False)
    step(-1, 1, nchunks > 1, True, False)
    n_full = max(nchunks - 2, 0)

    def one(c, carry):
        lax.cond(c % 2 == 0, lambda: step(c, 0, True, True, True), lambda: step(c, 1, True, True, True))
        return carry

    lax.fori_loop(0, 2 * (n_full // 2), one, 0)
    for c in range(2 * (n_full // 2), nchunks):
        step(c, c % 2, c + 2 < nchunks, c + 1 < nchunks, True)
    out = (acc_ref[0:hd, :] / acc_ref[hd:hd + 1, :]).T
    for g in range(groups):
        o_ref[:, g * hd:(g + 1) * hd] = out[g * tq:(g + 1) * tq, :].astype(o_ref.dtype)


FLASH_ONES_ROWS = 8


def _flash(q, k, v, hd):
    b, sq, hq = q.shape
    t, hkv = k.shape[1], k.shape[2]
    nkv = hkv // hd
    groups = hq // hkv
    tq = _tile(sq, 256)
    tk = _tile(t, 768)
    nchunks = t // tk
    gw = groups * hd
    hv = hd + FLASH_ONES_ROWS
    vt = v.reshape(b, nchunks, tk, nkv, hd).transpose(0, 3, 1, 4, 2)
    ones = jnp.zeros((b, nkv, nchunks, FLASH_ONES_ROWS, tk), BF16).at[:, :, :, 0, :].set(1.0)
    vt = jnp.concatenate([vt, ones], axis=3)
    return pl.pallas_call(
        functools.partial(_flash_kernel, groups=groups, hd=hd, tk=tk, rb=_tile(tk, 256), nchunks=nchunks),
        grid=(b, nkv, sq // tq),
        in_specs=[
            pl.BlockSpec((None, tq, gw), lambda bi, h, i: (bi, i, h)),
            pl.BlockSpec((None, t, hd), lambda bi, h, i: (bi, 0, h)),
            pl.BlockSpec((None, None, nchunks, hv, tk), lambda bi, h, i: (bi, h, 0, 0, 0)),
        ],
        out_specs=pl.BlockSpec((None, tq, gw), lambda bi, h, i: (bi, i, h)),
        out_shape=jax.ShapeDtypeStruct((b, sq, hq), BF16),
        scratch_shapes=[
            pltpu.VMEM((groups * tq, hd), BF16),
            pltpu.VMEM((tk, groups * tq), F32),
            pltpu.VMEM((tk, groups * tq), F32),
            pltpu.VMEM((tk, groups * tq), BF16),
            pltpu.VMEM((tk, groups * tq), BF16),
        ] + [pltpu.VMEM((1, groups * tq), F32)] * 5 + [
            pltpu.VMEM((hv, groups * tq), F32),
        ],
        compiler_params=_params("parallel", "parallel", "arbitrary"),
        name="flash_attn",
    )(q, k, vt)


def _dwconv_kernel(prev_ref, cur_ref, next_ref, w_ref, b_ref, g_ref, beta_ref, o_ref, win_ref, sh_ref, y_ref,
                   *, width, nblk, rc, cc):
    ts, d = cur_ref.shape
    j = pl.program_id(1)
    first = CONV_HALO - width // 2
    span = sh_ref.shape[1]
    win_ref[0:CONV_HALO, :] = jnp.where(j > 0, prev_ref[...], 0.0)
    win_ref[CONV_HALO:CONV_HALO + ts, :] = cur_ref[...]
    win_ref[CONV_HALO + ts:, :] = jnp.where(j < nblk - 1, next_ref[...], 0.0)
    for sft in range(1, SUBLANES):
        sh_ref[sft - 1] = win_ref[sft:sft + span, :]

    def rows_chunk(r, carry):
        r0 = pl.multiple_of(r * rc, rc)
        for c0 in range(0, d, cc):
            acc = jnp.zeros((rc // SUBLANES, SUBLANES, cc), F32)
            for sft in range(SUBLANES):
                taps = [k for k in range(width) if (first + k) % SUBLANES == sft]
                q_lo, q_hi = (first + taps[0]) // SUBLANES, (first + taps[-1]) // SUBLANES
                src = win_ref if sft == 0 else sh_ref.at[sft - 1]
                nrow = rc + (q_hi - q_lo) * SUBLANES
                big = src[pl.ds(r0 + q_lo * SUBLANES, nrow), c0:c0 + cc].reshape(nrow // SUBLANES, SUBLANES, cc)
                for k in taps:
                    g0 = (first + k) // SUBLANES - q_lo
                    acc = acc + big[g0:g0 + rc // SUBLANES] * w_ref[k, :, c0:c0 + cc]
            y_ref[pl.ds(r0, rc), c0:c0 + cc] = acc.reshape(rc, cc)
        return carry

    lax.fori_loop(0, ts // rc, rows_chunk, 0)
    y = y_ref[...] + b_ref[...]
    mu = jnp.mean(y, axis=-1, keepdims=True)
    yc = y - mu
    z = yc * lax.rsqrt(jnp.mean(yc * yc, axis=-1, keepdims=True) + EPS) * g_ref[...] + beta_ref[...]
    o_ref[...] = (z * jax.nn.sigmoid(z)).astype(o_ref.dtype)


def _dwconv_ln_silu(u, w_dw, b_dw, ln_g, ln_b):
    b, s, d = u.shape
    width = w_dw.shape[0]
    ts = _tile(s, 128)
    hb = ts // CONV_HALO
    nblk = s // ts
    last_h = s // CONV_HALO - 1
    last_tap = CONV_HALO - width // 2 + width - 1
    span = ts + SUBLANES * (last_tap // SUBLANES)
    vec = lambda a: a.reshape(1, d)
    row = pl.BlockSpec((1, d), lambda bi, j: (0, 0))
    return pl.pallas_call(
        functools.partial(_dwconv_kernel, width=width, nblk=nblk, rc=_tile(ts, 32), cc=_tile(d, 256)),
        grid=(b, nblk),
        in_specs=[
            pl.BlockSpec((None, CONV_HALO, d), lambda bi, j: (bi, jnp.maximum(j * hb - 1, 0), 0)),
            pl.BlockSpec((None, ts, d), lambda bi, j: (bi, j, 0)),
            pl.BlockSpec((None, CONV_HALO, d), lambda bi, j: (bi, jnp.minimum((j + 1) * hb, last_h), 0)),
            pl.BlockSpec((width, SUBLANES, d), lambda bi, j: (0, 0, 0)),
            row, row, row,
        ],
        out_specs=pl.BlockSpec((None, ts, d), lambda bi, j: (bi, j, 0)),
        out_shape=jax.ShapeDtypeStruct((b, s, d), BF16),
        scratch_shapes=[pltpu.VMEM((ts + 2 * CONV_HALO, d), F32), pltpu.VMEM((SUBLANES - 1, span, d), F32),
                        pltpu.VMEM((ts, d), F32)],
        compiler_params=_params("parallel", "arbitrary"),
        name="dwconv_ln_silu",
    )(u, u, u, jnp.broadcast_to(w_dw[:, None, :], (width, SUBLANES, d)), vec(b_dw), vec(ln_g), vec(ln_b))


def _rope_tables(seq, hd):
    half = hd // 2
    t = jnp.arange(seq)
    row = (t // GRID_W).astype(F32)
    col = (t % GRID_W).astype(F32)
    inv_freq = 1.0 / (ROPE_THETA ** (jnp.arange(0, half, 2, dtype=F32) / half))
    ang_r = row[:, None] * inv_freq[None, :]
    ang_c = col[:, None] * inv_freq[None, :]
    ang = jnp.concatenate([ang_r, ang_r, ang_c, ang_c], axis=-1)
    return jnp.cos(ang), jnp.sin(ang)


def _ffn_half(x, h, mods, rows, j0, weights, next_norm, cast=None):
    wg, wu, wd = weights
    res = _swiglu(h, wg, wu, cast=cast)
    act, cast_out = (res[0], tuple(res[1:])) if cast is not None else (res, None)
    x_new, h_next = _resid_norm(act, wd, x, mods, rows, j0 + 2, FFN_RESIDUAL, norm=next_norm)
    return x_new, h_next, cast_out


def kernel(x, c, ctx, c_ctx, ada_w, ada_b, norm_g, ffn_w_gate, ffn_w_up, ffn_w_down, attn_wq, attn_wk,
           attn_wv, attn_wo, attn_q_g, attn_k_g, conv_w1, conv_b1, conv_w_dw, conv_b_dw, conv_ln_g,
           conv_ln_b, conv_w2, conv_b2):
    B, S, D = x.shape
    C = ctx.shape[1]
    depth = ada_w.shape[0]
    hd = attn_q_g.shape[-1]
    n_mixers = 2

    ffn_f32 = (ffn_w_gate, ffn_w_up, ffn_w_down)
    ffn_order = [(i, s) for i in range(depth) for s in range(2)]
    ffn_bf16 = {ffn_order[0]: tuple(w[ffn_order[0]].astype(BF16) for w in ffn_f32)}
    wq_all, wk_all, wv_all, wo_all = (w.astype(BF16) for w in (attn_wq, attn_wk, attn_wv, attn_wo))
    w1_all, w2_all = conv_w1.astype(BF16), conv_w2.astype(BF16)

    n_rows = B + 1
    pad = (-n_rows) % 8
    c_all = jnp.concatenate([c, c_ctx[None, :], jnp.zeros((pad, D), F32)], axis=0)
    mods_all = _ada(c_all, ada_w, ada_b)
    mods_of = lambda i: mods_all[i, :n_rows].reshape(n_rows * N_ADA, 1, D)

    lat = _Rows(B * S, S, 0)
    cx = _Rows(B * C, B * C, B)
    rope = _rope_tables(S, hd)
    attn_scale = float(hd) ** -0.5 * math.log2(math.e)

    h_lat = x.reshape(B * S, D)
    h_ctx = ctx.reshape(B * C, D)
    hn_lat = _norm_mod(h_lat, norm_g[0, 0], mods_of(0), lat, 0)
    hn_ctx = _norm_mod(h_ctx, norm_g[0, 0], mods_of(0), cx, 0)
    for i in range(depth):
        kind, j = i % n_mixers, i // n_mixers
        last = i == depth - 1
        run_ctx = (not last) or kind == 0
        mods = mods_of(i)
        g = norm_g[i]

        def ffn(s, x_, h_, rows_, next_norm):
            pos = ffn_order.index((i, s))
            nxt_key = ffn_order[pos + 1] if rows_ is lat and pos + 1 < len(ffn_order) else None
            cast = None if nxt_key is None else tuple((w, nxt_key) for w in ffn_f32)
            x_new, h_next, cast_out = _ffn_half(x_, h_, mods, rows_, 6 * s, ffn_bf16[(i, s)], next_norm, cast=cast)
            if nxt_key is not None:
                ffn_bf16[nxt_key] = cast_out
            return x_new, h_next

        h_lat, a_lat = ffn(0, h_lat, hn_lat, lat, (g[1], mods, 3))
        if run_ctx:
            h_ctx, a_ctx = ffn(0, h_ctx, hn_ctx, cx, (g[1], mods, 3))

        post = (g[2], mods, 6)
        if kind == 0:
            q = _proj(a_lat, wq_all[j], hd, S, g=attn_q_g[j], rope=rope, scale=attn_scale)
            k = _proj(a_lat, wk_all[j], hd, S, g=attn_k_g[j], rope=rope)
            v = _proj(a_lat, wv_all[j], hd, S)
            kc = _proj(a_ctx, wk_all[j], hd, C, g=attn_k_g[j])
            vc = _proj(a_ctx, wv_all[j], hd, C)
            nkv = k.shape[1]
            k_all = jnp.concatenate([k.reshape(B, S, nkv), kc.reshape(B, C, nkv)], axis=1)
            v_all = jnp.concatenate([v.reshape(B, S, nkv), vc.reshape(B, C, nkv)], axis=1)
            o = _flash(q.reshape(B, S, -1), k_all, v_all, hd).reshape(B * S, -1)
            h_lat, hn_lat = _resid_norm(o, wo_all[j], h_lat, mods, lat, 5, 1.0, norm=post)
            if not last:
                qc = _proj(a_ctx, wq_all[j], hd, C, g=attn_q_g[j], scale=attn_scale)
                oc = _flash(qc.reshape(B, C, -1), kc.reshape(B, C, nkv), vc.reshape(B, C, nkv), hd)
                h_ctx, hn_ctx = _resid_norm(oc.reshape(B * C, -1), wo_all[j], h_ctx, mods, cx, 5, 1.0,
                                            norm=post)
        else:
            def conv_mixer(a, n_seq, seq):
                u = _glu(a, w1_all[j], conv_b1[j], F32)
                return _dwconv_ln_silu(u.reshape(n_seq, seq, D), conv_w_dw[j], conv_b_dw[j],
                                       conv_ln_g[j], conv_ln_b[j]).reshape(n_seq * seq, D)

            h_lat, hn_lat = _resid_norm(conv_mixer(a_lat, B, S), w2_all[j], h_lat, mods, lat, 5, 1.0,
                                        bias=conv_b2[j], norm=post)
            if not last:
                h_ctx, hn_ctx = _resid_norm(conv_mixer(a_ctx, B, C), w2_all[j], h_ctx, mods, cx, 5, 1.0,
                                            bias=conv_b2[j], norm=post)

        nxt = None if last else (norm_g[i + 1, 0], mods_of(i + 1), 0)
        h_lat, hn_lat = ffn(1, h_lat, hn_lat, lat, nxt)
        if not last:
            h_ctx, hn_ctx = ffn(1, h_ctx, hn_ctx, cx, nxt)
    return h_lat.reshape(B, S, D)
```
